```python
import math
import jax, jax.numpy as jnp
from jax import lax
import numpy as np

D_MODEL = 1024
BATCH = 8
SEQ = 2048
DEPTH = 2
DEC_BATCH = 16
DEC_SEQ = 4096
PAST_LEN = 128

CHUNK = 128
A_WIDTH = 1536
A_GROUPS = 12
A_GROUP_CH = A_WIDTH // A_GROUPS
B_WIDTH = 1024
HY_ORDER = 2
POS_EMB = 33
FILTER_HIDDEN = 64
DECAY_TARGET = 1e-2
FAST_DECAY_PCT = 0.3
SLOW_DECAY_PCT = 1.5
FILTER_EPS = 1e-6
IN_COLS = 2 * A_WIDTH + 3 * B_WIDTH + 2 * D_MODEL
P_HEADS = 8
P_DK = 256
N_KEYS = 128
N_EXPERTS = N_KEYS * N_KEYS
P_TOPK = 16
P_BLOCK = 128
RMS_EPS = 1e-6
LN_EPS = 1e-5

kernel_name = "hybrid_gmlp_hyena_peer_encoder"


def rms_norm(x, g):
    xf = x.astype(jnp.float32)
    y = xf * lax.rsqrt(jnp.mean(xf * xf, axis=-1, keepdims=True) + RMS_EPS)
    return (y * g.astype(jnp.float32)).astype(x.dtype)


def layer_norm(x, g, b):
    xf = x.astype(jnp.float32)
    mu = jnp.mean(xf, axis=-1, keepdims=True)
    xc = xf - mu
    y = xc * lax.rsqrt(jnp.mean(xc * xc, axis=-1, keepdims=True) + LN_EPS)
    return (y * g.astype(jnp.float32) + b.astype(jnp.float32)).astype(x.dtype)


def gmlp_branch(z, ln_g, ln_b, ws, bs):
    B, L, _ = z.shape
    a = jax.nn.gelu(z, approximate=False)
    u, v = jnp.split(a, 2, axis=-1)
    v = layer_norm(v, ln_g, ln_b)
    v = v.reshape(B, L // CHUNK, CHUNK, A_GROUPS, A_GROUP_CH)
    mixed = jnp.einsum('bcpgd,gqp->bcqgd', v, ws) + bs.T[None, None, :, :, None]
    return u * mixed.reshape(B, L, A_WIDTH)


def short_conv(x, w, b):
    xp = jnp.pad(x, ((0, 0), (1, 1), (0, 0)))
    return xp[:, :-2] * w[0] + xp[:, 1:-1] * w[1] + xp[:, 2:] * w[2] + b


def hyena_filters(L, w1, b1, fr1, w2, b2, fr2, w3, b3, fr3, w4):
    f32 = jnp.float32
    t = jnp.linspace(0.0, 1.0, L, dtype=f32)[:, None]
    bands = (POS_EMB - 1) // 2
    wpos = 2.0 * math.pi * jnp.arange(L, dtype=f32)[:, None] / L
    freqs = jnp.linspace(1e-4, bands - 1, bands, dtype=f32)[None, :]
    feat = jnp.concatenate([t, jnp.cos(freqs * wpos), -jnp.sin(freqs * wpos)], axis=-1)
    h = jnp.sin(fr1.astype(f32) * (feat @ w1.astype(f32) + b1.astype(f32)))
    h = jnp.sin(fr2.astype(f32) * (h @ w2.astype(f32) + b2.astype(f32)))
    h = jnp.sin(fr3.astype(f32) * (h @ w3.astype(f32) + b3.astype(f32)))
    h = (h @ w4.astype(f32)).reshape(L, HY_ORDER, 2, B_WIDTH)
    max_decay = math.log(DECAY_TARGET) / FAST_DECAY_PCT
    min_decay = math.log(DECAY_TARGET) / SLOW_DECAY_PCT
    deltas = jnp.linspace(min_decay, max_decay, B_WIDTH, dtype=f32)
    h = h * jnp.exp(-t[:, :, None, None] * jnp.abs(deltas))
    h_fwd = h[:, :, 0]
    h_bwd = h[1:, :, 1]
    h_fwd = h_fwd / (jnp.sum(jnp.abs(h_fwd), axis=0, keepdims=True) + FILTER_EPS)
    h_bwd = h_bwd / (jnp.sum(jnp.abs(h_bwd), axis=0, keepdims=True) + FILTER_EPS)
    k = jnp.concatenate([h_fwd, jnp.zeros((1, HY_ORDER, B_WIDTH), f32), h_bwd[::-1]], axis=0)
    return jnp.fft.rfft(k, axis=0)


def long_conv(z, k_f):
    L = z.shape[1]
    zf = jnp.fft.rfft(z, n=2 * L, axis=1)
    return jnp.fft.irfft(zf * k_f[None], n=2 * L, axis=1)[:, :L]


def hyena_branch(z, conv_w, conv_b, w1, b1, fr1, w2, b2, fr2, w3, b3, fr3, w4, skip):
    L = z.shape[1]
    k_f = hyena_filters(L, w1, b1, fr1, w2, b2, fr2, w3, b3, fr3, w4)
    zc = short_conv(z, conv_w, conv_b).astype(jnp.float32)
    v, x1, x2 = jnp.split(zc, 3, axis=-1)
    skip = skip.astype(jnp.float32)
    s = x1 * (long_conv(v, k_f[:, 0]) + v * skip[0])
    y = x2 * (long_conv(s, k_f[:, 1]) + s * skip[1])
    return y.astype(z.dtype)


def peer(x, wq, keys, u_tab, v_tab):
    B, L, D = x.shape
    xb = x.reshape(-1, P_BLOCK, D)

    def block(xt):
        q = (xt @ wq).reshape(P_BLOCK, P_HEADS, 2, P_DK // 2)
        s = jnp.einsum('thpd,hpnd->thpn', q, keys).astype(jnp.float32)
        top_s, top_i = lax.top_k(s, P_TOPK)
        cand_s = (top_s[:, :, 0, :, None] + top_s[:, :, 1, None, :]).reshape(P_BLOCK, P_HEADS, P_TOPK * P_TOPK)
        cand_i = (top_i[:, :, 0, :, None] * N_KEYS + top_i[:, :, 1, None, :]).reshape(P_BLOCK, P_HEADS, P_TOPK * P_TOPK)
        best_s, best_pos = lax.top_k(cand_s, P_TOPK)
        idx = jnp.take_along_axis(cand_i, best_pos, axis=-1)
        gate = jax.nn.softmax(best_s, axis=-1)
        h = jnp.einsum('td,thkd->thk', xt, u_tab[idx]).astype(jnp.float32)
        a = (gate * jax.nn.gelu(h, approximate=False)).astype(xt.dtype)
        return jnp.einsum('thk,thkd->td', a, v_tab[idx])

    return lax.map(block, xb).reshape(B, L, D)


def _trunk(x, norm_mix, w_in, gm_ln_g, gm_ln_b, gm_ws, gm_bs, hy_conv_w, hy_conv_b,
           hy_f_w1, hy_f_b1, hy_f_freq1, hy_f_w2, hy_f_b2, hy_f_freq2, hy_f_w3, hy_f_b3, hy_f_freq3,
           hy_f_w4, hy_skip, w_a_out, w_b_out, w_o, norm_ffn, peer_wq, peer_keys, peer_u, peer_v, final_norm):
    for l in range(DEPTH):
        h = rms_norm(x, norm_mix[l])
        proj = h @ w_in[l]
        z_a, z_b, z_g = jnp.split(proj, [2 * A_WIDTH, 2 * A_WIDTH + 3 * B_WIDTH], axis=-1)
        y_a = gmlp_branch(z_a, gm_ln_g[l], gm_ln_b[l], gm_ws[l], gm_bs[l]) @ w_a_out[l]
        y_b = hyena_branch(z_b, hy_conv_w[l], hy_conv_b[l],
                           hy_f_w1[l], hy_f_b1[l], hy_f_freq1[l],
                           hy_f_w2[l], hy_f_b2[l], hy_f_freq2[l],
                           hy_f_w3[l], hy_f_b3[l], hy_f_freq3[l],
                           hy_f_w4[l], hy_skip[l]) @ w_b_out[l]
        g_a, g_b = jnp.split(jax.nn.sigmoid(z_g), 2, axis=-1)
        x = x + (g_a * y_a + g_b * y_b) @ w_o[l]
        x = x + peer(rms_norm(x, norm_ffn[l]), peer_wq[l], peer_keys[l], peer_u[l], peer_v[l])
    return rms_norm(x, final_norm)


def setup_inputs(seed: int = 0) -> dict:
    key = jax.random.key(seed)
    ks = jax.random.split(key, 32)
    f32 = jnp.float32

    def nrm(k, shape, scale):
        return jax.random.normal(k, shape, f32) * scale

    def gain(k, shape):
        return 1.0 + 0.02 * jax.random.normal(k, shape, f32)

    return {
        "x_prompt": nrm(ks[0], (BATCH, SEQ, D_MODEL), 1.0),
        "x_sample": nrm(ks[1], (DEC_BATCH, DEC_SEQ, D_MODEL), 1.0),
        "norm_mix": gain(ks[2], (DEPTH, D_MODEL)),
        "w_in": nrm(ks[3], (DEPTH, D_MODEL, IN_COLS), D_MODEL ** -0.5),
        "gm_ln_g": gain(ks[4], (DEPTH, A_WIDTH)),
        "gm_ln_b": nrm(ks[5], (DEPTH, A_WIDTH), 0.02),
        "gm_ws": nrm(ks[6], (DEPTH, A_GROUPS, CHUNK, CHUNK), CHUNK ** -0.5),
        "gm_bs": gain(ks[7], (DEPTH, A_GROUPS, CHUNK)),
        "hy_conv_w": nrm(ks[8], (DEPTH, 3, 3 * B_WIDTH), 3 ** -0.5),
        "hy_conv_b": nrm(ks[9], (DEPTH, 3 * B_WIDTH), 0.02),
        "hy_f_w1": nrm(ks[10], (DEPTH, POS_EMB, FILTER_HIDDEN), POS_EMB ** -0.5),
        "hy_f_b1": nrm(ks[11], (DEPTH, FILTER_HIDDEN), 0.02),
        "hy_f_freq1": gain(ks[12], (DEPTH, FILTER_HIDDEN)),
        "hy_f_w2": nrm(ks[13], (DEPTH, FILTER_HIDDEN, FILTER_HIDDEN), FILTER_HIDDEN ** -0.5),
        "hy_f_b2": nrm(ks[14], (DEPTH, FILTER_HIDDEN), 0.02),
        "hy_f_freq2": gain(ks[15], (DEPTH, FILTER_HIDDEN)),
        "hy_f_w3": nrm(ks[16], (DEPTH, FILTER_HIDDEN, FILTER_HIDDEN), FILTER_HIDDEN ** -0.5),
        "hy_f_b3": nrm(ks[17], (DEPTH, FILTER_HIDDEN), 0.02),
        "hy_f_freq3": gain(ks[18], (DEPTH, FILTER_HIDDEN)),
        "hy_f_w4": nrm(ks[19], (DEPTH, FILTER_HIDDEN, HY_ORDER * 2 * B_WIDTH), FILTER_HIDDEN ** -0.5),
        "hy_skip": nrm(ks[20], (DEPTH, HY_ORDER, B_WIDTH), 1.0),
        "w_a_out": nrm(ks[21], (DEPTH, A_WIDTH, D_MODEL), A_WIDTH ** -0.5),
        "w_b_out": nrm(ks[22], (DEPTH, B_WIDTH, D_MODEL), B_WIDTH ** -0.5),
        "w_o": nrm(ks[23], (DEPTH, D_MODEL, D_MODEL), D_MODEL ** -0.5),
        "norm_ffn": gain(ks[24], (DEPTH, D_MODEL)),
        "peer_wq": nrm(ks[25], (DEPTH, D_MODEL, P_HEADS * P_DK), D_MODEL ** -0.5),
        "peer_keys": nrm(ks[26], (DEPTH, P_HEADS, 2, N_KEYS, P_DK // 2), (P_DK // 2) ** -0.5),
        "peer_u": nrm(ks[27], (DEPTH, N_EXPERTS, D_MODEL), D_MODEL ** -0.5),
        "peer_v": nrm(ks[28], (DEPTH, N_EXPERTS, D_MODEL), D_MODEL ** -0.5),
        "final_norm": gain(ks[29], (D_MODEL,)),
    }


def reference(x_prompt, x_sample, norm_mix, w_in, gm_ln_g, gm_ln_b, gm_ws, gm_bs, hy_conv_w, hy_conv_b,
              hy_f_w1, hy_f_b1, hy_f_freq1, hy_f_w2, hy_f_b2, hy_f_freq2, hy_f_w3, hy_f_b3, hy_f_freq3,
              hy_f_w4, hy_skip, w_a_out, w_b_out, w_o, norm_ffn, peer_wq, peer_keys, peer_u, peer_v, final_norm):
    weights = (norm_mix, w_in, gm_ln_g, gm_ln_b, gm_ws, gm_bs, hy_conv_w, hy_conv_b,
               hy_f_w1, hy_f_b1, hy_f_freq1, hy_f_w2, hy_f_b2, hy_f_freq2, hy_f_w3, hy_f_b3, hy_f_freq3,
               hy_f_w4, hy_skip, w_a_out, w_b_out, w_o, norm_ffn, peer_wq, peer_keys, peer_u, peer_v, final_norm)
    y_prompt = _trunk(x_prompt, *weights)
    y_sample = _trunk(x_sample, *weights)
    return (y_prompt, y_sample)
```

```python
import functools
import math

import numpy as np
import jax
import jax.numpy as jnp
from jax import lax
from jax.experimental import pallas as pl
from jax.experimental.pallas import tpu as pltpu

F32 = jnp.float32
BF16 = jnp.bfloat16

LANE = 128
VMEM_LIMIT_BYTES = 56 * 1024 * 1024

CHUNK = 128
A_GROUPS = 12
HY_ORDER = 2
POS_EMB = 33
DECAY_TARGET = 1e-2
FAST_DECAY_PCT = 0.3
SLOW_DECAY_PCT = 1.5
FILTER_EPS = 1e-6
P_HEADS = 8
N_KEYS = 128
P_TOPK = 16
RMS_EPS = 1e-6
LN_EPS = 1e-5

FFT_RADIX = 16
FFT_HALF = FFT_RADIX // 2
FFT_NF1 = FFT_HALF + 1

MIX_TOKENS = 256
HY_CH = LANE
FILTER_ROWS = 256
ROUTE_TOKENS = 256
PEER_TOKENS = 512
PEER_KEYS_PER_STEP = 8

NEG_INF = float("-inf")


def _dot(a, b):
    return jnp.dot(a, b, preferred_element_type=F32)


def _dot_hi(a, b):
    return jnp.dot(a, b, preferred_element_type=F32, precision=lax.Precision.HIGHEST)


def _gelu(x):
    return 0.5 * x * (1.0 + lax.erf(x * math.sqrt(0.5)))


def _rms(x, g):
    return x * lax.rsqrt(jnp.mean(x * x, axis=-1, keepdims=True) + RMS_EPS) * g


def _resident(shape):
    zeros = (0,) * len(shape)
    return pl.BlockSpec(shape, lambda *_: zeros, pipeline_mode=pl.Buffered(1))


def _params(*sem):
    return pltpu.CompilerParams(dimension_semantics=sem, vmem_limit_bytes=VMEM_LIMIT_BYTES)


def _mix_in_kernel(x_ref, g_ref, win_ref, lng_ref, lnb_ref, ws_ref, bsb_ref, waout_ref,
                   yag_ref, gb_ref, zb_ref, t_ref, *, a_width, b_cols):
    x = x_ref[...]
    hb = _rms(x, g_ref[...]).astype(BF16)
    u = _gelu(_dot(hb, win_ref[:, 0:a_width]))
    v = _gelu(_dot(hb, win_ref[:, a_width:2 * a_width]))
    mu = jnp.mean(v, axis=-1, keepdims=True)
    vc = v - mu
    vn = vc * lax.rsqrt(jnp.mean(vc * vc, axis=-1, keepdims=True) + LN_EPS)
    vb = (vn * lng_ref[...] + lnb_ref[...]).astype(BF16)
    gch = a_width // A_GROUPS
    for c in range(x.shape[0] // CHUNK):
        rows = slice(c * CHUNK, (c + 1) * CHUNK)
        for g in range(A_GROUPS):
            cols = slice(g * gch, (g + 1) * gch)
            mixed = _dot(ws_ref[g], vb[rows, cols]) + bsb_ref[g]
            t_ref[rows, cols] = (u[rows, cols] * mixed).astype(BF16)
    y_a = _dot(t_ref[...], waout_ref[...])
    d = x.shape[1]
    z_g = _dot(hb, win_ref[:, 2 * a_width + b_cols:2 * a_width + b_cols + 2 * d])
    gate = jax.nn.sigmoid(z_g)
    yag_ref[...] = gate[:, 0:d] * y_a
    gb_ref[...] = gate[:, d:2 * d]
    zb_ref[...] = _dot(hb, win_ref[:, 2 * a_width:2 * a_width + b_cols])


def _mix_in(x, g, w_in, ln_g, ln_b, ws, bsb, w_a_out):
    t, d = x.shape
    a_width = w_a_out.shape[0]
    b_cols = w_in.shape[1] - 2 * a_width - 2 * d
    tb = MIX_TOKENS
    tok = lambda cols: pl.BlockSpec((tb, cols), lambda i: (i, 0))
    return pl.pallas_call(
        functools.partial(_mix_in_kernel, a_width=a_width, b_cols=b_cols),
        grid=(t // tb,),
        in_specs=[tok(d), _resident(g.shape), _resident(w_in.shape), _resident(ln_g.shape),
                  _resident(ln_b.shape), _resident(ws.shape), _resident(bsb.shape),
                  _resident(w_a_out.shape)],
        out_specs=[tok(d), tok(d), tok(b_cols)],
        out_shape=[jax.ShapeDtypeStruct((t, d), F32), jax.ShapeDtypeStruct((t, d), F32),
                   jax.ShapeDtypeStruct((t, b_cols), F32)],
        scratch_shapes=[pltpu.VMEM((tb, a_width), BF16)],
        compiler_params=_params("parallel"),
        name="mix_in",
    )(x, g, w_in, ln_g, ln_b, ws, bsb, w_a_out)


def _fft_consts(seq):
    n = 2 * seq
    n2 = n // FFT_RADIX
    k = np.arange(n2)
    ang = 2.0 * np.pi * np.outer(k, k) / n2
    dr, di = np.cos(ang), -np.sin(ang)
    m_fwd = np.block([[dr, -di], [di, dr]])
    m_inv = np.block([[dr, di], [-di, dr]])
    tw_ang = 2.0 * np.pi * np.arange(FFT_NF1)[:, None] * np.arange(n2)[None, :] / n
    bc = lambda a: np.broadcast_to(a[:, :, None], (FFT_NF1, n2, HY_CH))
    return m_fwd, m_inv, bc(np.cos(tw_ang)), bc(np.sin(tw_ang))


def _axpy(acc, c, x):
    if abs(c) < 1e-12:
        return acc
    if abs(abs(c) - 1.0) < 1e-12:
        if acc is None:
            return x if c > 0 else -x
        return acc + x if c > 0 else acc - x
    return c * x if acc is None else acc + c * x


def _fft_fwd(x, twc_ref, tws_ref, matmul, dtype):
    n2 = x.shape[0] // FFT_HALF
    slabs = [x[s * n2:(s + 1) * n2] for s in range(FFT_HALF)]
    a_re, a_im = [], []
    for f1 in range(FFT_NF1):
        re = im = None
        for s1 in range(FFT_HALF):
            th = 2.0 * math.pi * s1 * f1 / FFT_RADIX
            re = _axpy(re, math.cos(th), slabs[s1])
            im = _axpy(im, -math.sin(th), slabs[s1])
        c, s = twc_ref[f1], tws_ref[f1]
        if im is None:
            a_re.append((re * c).astype(dtype))
            a_im.append((-re * s).astype(dtype))
        else:
            a_re.append((re * c + im * s).astype(dtype))
            a_im.append((im * c - re * s).astype(dtype))
    a = jnp.concatenate([jnp.concatenate(a_re, axis=1), jnp.concatenate(a_im, axis=1)], axis=0)
    return matmul(a)


def _fft_inv(y, twc_ref, tws_ref, matmul):
    n2 = y.shape[0] // 2
    ch = y.shape[1] // FFT_NF1
    n = FFT_RADIX * n2
    c_all = matmul(y)
    c_re, c_im = [], []
    for f1 in range(FFT_NF1):
        r = c_all[:n2, f1 * ch:(f1 + 1) * ch]
        i = c_all[n2:, f1 * ch:(f1 + 1) * ch]
        c, s = twc_ref[f1], tws_ref[f1]
        c_re.append(r * c - i * s)
        c_im.append(i * c + r * s)
    out = []
    for t1 in range(FFT_HALF):
        acc = None
        for f1 in range(FFT_NF1):
            wgt = (1.0 if f1 in (0, FFT_HALF) else 2.0) / n
            th = 2.0 * math.pi * t1 * f1 / FFT_RADIX
            acc = _axpy(acc, wgt * math.cos(th), c_re[f1])
            acc = _axpy(acc, -wgt * math.sin(th), c_im[f1])
        out.append(acc)
    return jnp.concatenate(out, axis=0)


def _spec_mul(x, k):
    n2 = x.shape[0] // 2
    xr, xi, kr, ki = x[:n2], x[n2:], k[:n2], k[n2:]
    return jnp.concatenate([xr * kr - xi * ki, xr * ki + xi * kr], axis=0)


def _filter_time_kernel(feat_ref, w1_ref, b1_ref, fr1_ref, w2_ref, b2_ref, fr2_ref, w3_ref, b3_ref,
                        fr3_ref, w4_ref, absd_ref, bwd_ref, h_ref, norm_ref, *, seq):
    i = pl.program_id(0)
    h = jnp.sin(fr1_ref[...] * (_dot_hi(feat_ref[...], w1_ref[...]) + b1_ref[...]))
    h = jnp.sin(fr2_ref[...] * (_dot_hi(h, w2_ref[...]) + b2_ref[...]))
    h = jnp.sin(fr3_ref[...] * (_dot_hi(h, w3_ref[...]) + b3_ref[...]))
    rows, cols = h_ref.shape
    pos = lax.broadcasted_iota(jnp.int32, (rows, cols), 0) + i * rows
    t = pos.astype(F32) / float(seq - 1)
    hw = _dot_hi(h, w4_ref[...]) * jnp.exp(-t * absd_ref[...])
    hw = jnp.where(jnp.logical_and(pos == 0, bwd_ref[...] > 0.0), 0.0, hw)
    h_ref[...] = hw

    @pl.when(i == 0)
    def _():
        norm_ref[...] = jnp.zeros_like(norm_ref)

    norm_ref[...] += jnp.sum(jnp.abs(hw), axis=0, keepdims=True)


def _filter_time(feat, w1, b1, fr1, w2, b2, fr2, w3, b3, fr3, w4, absd, bwd):
    seq = feat.shape[0]
    cols = w4.shape[1]
    rows = FILTER_ROWS
    small = [w1, b1, fr1, w2, b2, fr2, w3, b3, fr3, w4, absd, bwd]
    return pl.pallas_call(
        functools.partial(_filter_time_kernel, seq=seq),
        grid=(seq // rows,),
        in_specs=[pl.BlockSpec((rows, feat.shape[1]), lambda i: (i, 0))]
        + [_resident(a.shape) for a in small],
        out_specs=[pl.BlockSpec((rows, cols), lambda i: (i, 0)),
                   pl.BlockSpec((1, cols), lambda i: (0, 0))],
        out_shape=[jax.ShapeDtypeStruct((seq, cols), F32), jax.ShapeDtypeStruct((1, cols), F32)],
        compiler_params=_params("arbitrary"),
        name="filter_time",
    )(feat, *small)


def _filter_spec_kernel(h_ref, norm_ref, mhi_ref, mlo_ref, twc_ref, tws_ref, kf_ref):
    direction = pl.program_id(2)
    x = h_ref[...] / (norm_ref[...] + FILTER_EPS)

    def matmul(a):
        a_hi = a.astype(BF16)
        a_lo = (a - a_hi.astype(F32)).astype(BF16)
        m_hi = mhi_ref[...]
        return _dot(m_hi, a_hi) + (_dot(m_hi, a_lo) + _dot(mlo_ref[...], a_hi))

    k = _fft_fwd(x, twc_ref, tws_ref, matmul, F32)
    n2 = k.shape[0] // 2

    @pl.when(direction == 0)
    def _():
        kf_ref[0, 0] = k

    @pl.when(direction == 1)
    def _():
        kf_ref[0, 0, 0:n2, :] += k[:n2]
        kf_ref[0, 0, n2:2 * n2, :] -= k[n2:]


def _filter_spec(h, norm, m_hi, m_lo, twc, tws):
    seq = h.shape[0]
    n2 = 2 * seq // FFT_RADIX
    ncb = h.shape[1] // (2 * HY_ORDER * HY_CH)
    col = lambda o, c, d: (0, (o * 2 + d) * ncb + c)
    return pl.pallas_call(
        _filter_spec_kernel,
        grid=(HY_ORDER, ncb, 2),
        in_specs=[pl.BlockSpec((seq, HY_CH), col), pl.BlockSpec((1, HY_CH), col),
                  _resident(m_hi.shape), _resident(m_lo.shape),
                  _resident(twc.shape), _resident(tws.shape)],
        out_specs=pl.BlockSpec((1, 1, 2 * n2, FFT_NF1 * HY_CH), lambda o, c, d: (o, c, 0, 0)),
        out_shape=jax.ShapeDtypeStruct((HY_ORDER, ncb, 2 * n2, FFT_NF1 * HY_CH), F32),
        compiler_params=_params("parallel", "parallel", "arbitrary"),
        name="filter_spec",
    )(h, norm, m_hi, m_lo, twc, tws)


def _hyena_kernel(zv_ref, zx1_ref, zx2_ref, cwv_ref, cwx1_ref, cwx2_ref, cbv_ref, cbx1_ref,
                  cbx2_ref, skip_ref, kf_ref, mfwd_ref, minv_ref, twc_ref, tws_ref, y_ref):
    seq = zv_ref.shape[1]
    row = lax.broadcasted_iota(jnp.int32, (seq, zv_ref.shape[2]), 0)

    def short_conv(z_ref, cw_ref, cb_ref):
        z = z_ref[0]
        z_prev = jnp.where(row == 0, 0.0, pltpu.roll(z, 1, 0))
        z_next = jnp.where(row == seq - 1, 0.0, pltpu.roll(z, seq - 1, 0))
        return z_prev * cw_ref[0:1, :] + z * cw_ref[1:2, :] + z_next * cw_ref[2:3, :] + cb_ref[...]

    def long_conv(z, order):
        spec = _fft_fwd(z, twc_ref, tws_ref, lambda a: _dot(mfwd_ref[...], a), BF16)
        prod = _spec_mul(spec, kf_ref[order, 0]).astype(BF16)
        return _fft_inv(prod, twc_ref, tws_ref, lambda a: _dot(minv_ref[...], a))

    v = short_conv(zv_ref, cwv_ref, cbv_ref)
    s = short_conv(zx1_ref, cwx1_ref, cbx1_ref) * (long_conv(v, 0) + v * skip_ref[0:1, :])
    y = short_conv(zx2_ref, cwx2_ref, cbx2_ref) * (long_conv(s, 1) + s * skip_ref[1:2, :])
    y_ref[0] = y


def _hyena(zb, conv_w, conv_b, skip, kf, m_fwd, m_inv, twc, tws):
    batch, seq, cols = zb.shape
    width = cols // 3
    ncb = width // HY_CH
    zspec = lambda part: pl.BlockSpec((1, seq, HY_CH), lambda c, b: (b, 0, part * ncb + c))
    wspec = lambda part: pl.BlockSpec((3, HY_CH), lambda c, b: (0, part * ncb + c))
    bspec = lambda part: pl.BlockSpec((1, HY_CH), lambda c, b: (0, part * ncb + c))
    return pl.pallas_call(
        _hyena_kernel,
        grid=(ncb, batch),
        in_specs=[zspec(0), zspec(1), zspec(2), wspec(0), wspec(1), wspec(2),
                  bspec(0), bspec(1), bspec(2),
                  pl.BlockSpec((HY_ORDER, HY_CH), lambda c, b: (0, c)),
                  pl.BlockSpec((HY_ORDER, 1) + kf.shape[2:], lambda c, b: (0, c, 0, 0)),
                  _resident(m_fwd.shape), _resident(m_inv.shape),
                  _resident(twc.shape), _resident(tws.shape)],
        out_specs=pl.BlockSpec((1, seq, HY_CH), lambda c, b: (b, 0, c)),
        out_shape=jax.ShapeDtypeStruct((batch, seq, width), F32),
        compiler_params=_params("parallel", "arbitrary"),
        name="hyena",
    )(zb, zb, zb, conv_w, conv_w, conv_w, conv_b, conv_b, conv_b, skip, kf, m_fwd, m_inv, twc, tws)


def _merge_kernel(x_ref, yag_ref, gb_ref, yb_ref, wb_ref, wo_ref, g_ref, x2_ref, xnt_ref):
    y_b = _dot(yb_ref[...].astype(BF16), wb_ref[...])
    merged = yag_ref[...] + gb_ref[...] * y_b
    x2 = x_ref[...] + _dot(merged.astype(BF16), wo_ref[...])
    x2_ref[...] = x2
    xnt_ref[...] = _rms(x2, g_ref[...]).T


def _merge(x, yag, gb, yb, w_b_out, w_o, g):
    t, d = x.shape
    tb = MIX_TOKENS
    tok = lambda cols: pl.BlockSpec((tb, cols), lambda i: (i, 0))
    return pl.pallas_call(
        _merge_kernel,
        grid=(t // tb,),
        in_specs=[tok(d), tok(d), tok(d), tok(yb.shape[1]), _resident(w_b_out.shape),
                  _resident(w_o.shape), _resident(g.shape)],
        out_specs=[tok(d), pl.BlockSpec((d, tb), lambda i: (0, i))],
        out_shape=[jax.ShapeDtypeStruct((t, d), F32), jax.ShapeDtypeStruct((d, t), F32)],
        compiler_params=_params("parallel"),
        name="merge",
    )(x, yag, gb, yb, w_b_out, w_o, g)


def _top_values(s):
    width = s.shape[1]
    rank = lax.broadcasted_iota(jnp.int32, (P_TOPK, width), 0)
    top = jnp.full((P_TOPK, width), NEG_INF, F32)
    for a in range(P_TOPK):
        m = jnp.max(s, axis=0, keepdims=True)
        top = jnp.where(rank == a, m, top)
        s = jnp.where(s == m, NEG_INF, s)
    return top


_PAIR_ROWS = tuple(P_TOPK // (a + 1) for a in range(P_TOPK))


def _candidate_sums(t1, t2):
    width = t1.shape[1]
    row8 = lax.broadcasted_iota(jnp.int32, (8, width), 0)
    parts = [t1[0:1] + t2, t1[1:2] + t2[0:8]]
    for a in range(2, 8):
        parts.append(jnp.where(row8 < _PAIR_ROWS[a], t1[a:a + 1] + t2[0:8], NEG_INF))
    parts.append(t1[8:16] + t2[0:1])
    return jnp.concatenate(parts, axis=0)


def _routing_kernel(xnt_ref, wqt_ref, keys_ref, s2_ref, e2_ref, thr_ref, w1_ref, q_ref):
    q_ref[...] = _dot_hi(wqt_ref[...], xnt_ref[...])
    tokens = xnt_ref.shape[1]
    dk = keys_ref.shape[3]

    def head(h, carry):
        base = pl.multiple_of(h * 2 * dk, 2 * dk)
        s1_all = _dot_hi(keys_ref[h, 0], q_ref[pl.ds(base, dk), :])
        s2_all = _dot_hi(keys_ref[h, 1], q_ref[pl.ds(base + dk, dk), :])
        for w in range(tokens // LANE):
            lanes = slice(w * LANE, (w + 1) * LANE)
            s1, s2 = s1_all[:, lanes], s2_all[:, lanes]
            t1, t2 = _top_values(s1), _top_values(s2)
            cand = _candidate_sums(t1, t2)
            best = t1[0:1] + t2[0:1]
            rest = cand
            for _ in range(P_TOPK):
                tau = jnp.max(rest, axis=0, keepdims=True)
                rest = jnp.where(rest == tau, NEG_INF, rest)
            z = jnp.sum(jnp.where(cand >= tau, jnp.exp(cand - best), 0.0), axis=0, keepdims=True)
            s2_ref[h, :, lanes] = s2
            e2_ref[h, :, lanes] = jnp.exp(s2 - t2[0:1])
            thr_ref[h, :, lanes] = tau - s1
            w1_ref[h, :, lanes] = jnp.exp(s1 - t1[0:1]) / z
        return carry

    lax.fori_loop(0, P_HEADS, head, 0)


def _routing(xnt, wqt, keys):
    d, t = xnt.shape
    tb = ROUTE_TOKENS
    out = jax.ShapeDtypeStruct((P_HEADS, N_KEYS, t), F32)
    ospec = pl.BlockSpec((P_HEADS, N_KEYS, tb), lambda i: (0, 0, i))
    return pl.pallas_call(
        _routing_kernel,
        grid=(t // tb,),
        in_specs=[pl.BlockSpec((d, tb), lambda i: (0, i)), _resident(wqt.shape),
                  _resident(keys.shape)],
        out_specs=[ospec] * 4,
        out_shape=[out] * 4,
        scratch_shapes=[pltpu.VMEM((wqt.shape[0], tb), F32)],
        compiler_params=_params("parallel"),
        name="routing",
    )(xnt, wqt, keys)


def _experts_kernel(xnt_ref, s2_ref, e2_ref, thr_ref, w1_ref, u_ref, vt_ref, x2_ref, g_ref, o_ref,
                    xb_ref, acc_ref, ht_ref, at_ref, *, final_norm):
    k = pl.program_id(1)
    tokens = xnt_ref.shape[1]

    @pl.when(k == 0)
    def _():
        xb_ref[...] = xnt_ref[...].astype(BF16)
        acc_ref[...] = jnp.zeros_like(acc_ref)

    ht_ref[...] = _dot(u_ref[...], xb_ref[...])

    group = pl.ds(pl.multiple_of(k * PEER_KEYS_PER_STEP, PEER_KEYS_PER_STEP), PEER_KEYS_PER_STEP)
    group_row = lax.broadcasted_iota(jnp.int32, (PEER_KEYS_PER_STEP, LANE), 0)

    def key_rows(i, carry):
        rows = pl.ds(pl.multiple_of(i * N_KEYS, N_KEYS), N_KEYS)
        pick = group_row == i
        for w in range(tokens // LANE):
            lanes = slice(w * LANE, (w + 1) * LANE)
            gate = jnp.zeros((N_KEYS, LANE), F32)
            for h in range(P_HEADS):
                thr = jnp.sum(jnp.where(pick, thr_ref[h, group, lanes], 0.0), axis=0, keepdims=True)
                w1 = jnp.sum(jnp.where(pick, w1_ref[h, group, lanes], 0.0), axis=0, keepdims=True)
                gate = gate + jnp.where(s2_ref[h, :, lanes] >= thr, e2_ref[h, :, lanes], 0.0) * w1
            at_ref[rows, lanes] = (gate * _gelu(ht_ref[rows, lanes])).astype(BF16)
        return carry

    lax.fori_loop(0, PEER_KEYS_PER_STEP, key_rows, 0)
    acc_ref[...] += _dot(vt_ref[...], at_ref[...])

    @pl.when(k == pl.num_programs(1) - 1)
    def _():
        out = x2_ref[...] + acc_ref[...].T
        if final_norm:
            out = _rms(out, g_ref[...])
        o_ref[...] = out


def _experts(xnt, s2, e2, thr, w1, u, vt, x2, g, final_norm):
    d, t = xnt.shape
    tb = PEER_TOKENS
    eb = PEER_KEYS_PER_STEP * N_KEYS
    route = pl.BlockSpec((P_HEADS, N_KEYS, tb), lambda i, k: (0, 0, i))
    return pl.pallas_call(
        functools.partial(_experts_kernel, final_norm=final_norm),
        grid=(t // tb, u.shape[0] // eb),
        in_specs=[pl.BlockSpec((d, tb), lambda i, k: (0, i)), route, route, route, route,
                  pl.BlockSpec((eb, d), lambda i, k: (k, 0)),
                  pl.BlockSpec((d, eb), lambda i, k: (0, k)),
                  pl.BlockSpec((tb, d), lambda i, k: (i, 0)),
                  pl.BlockSpec(g.shape, lambda i, k: (0, 0))],
        out_specs=pl.BlockSpec((tb, d), lambda i, k: (i, 0)),
        out_shape=jax.ShapeDtypeStruct((t, d), F32),
        scratch_shapes=[pltpu.VMEM((d, tb), BF16), pltpu.VMEM((d, tb), F32),
                        pltpu.VMEM((eb, tb), F32), pltpu.VMEM((eb, tb), BF16)],
        compiler_params=_params("parallel", "arbitrary"),
        name="experts",
    )(xnt, s2, e2, thr, w1, u, vt, x2, g)


def _positional_features(seq):
    t = jnp.linspace(0.0, 1.0, seq, dtype=F32)[:, None]
    bands = (POS_EMB - 1) // 2
    wpos = 2.0 * math.pi * jnp.arange(seq, dtype=F32)[:, None] / seq
    freqs = jnp.linspace(1e-4, bands - 1, bands, dtype=F32)[None, :]
    feat = jnp.concatenate([t, jnp.cos(freqs * wpos), -jnp.sin(freqs * wpos)], axis=-1)
    return jnp.pad(feat, ((0, 0), (0, LANE - POS_EMB)))


def _decay_rates(width):
    max_decay = math.log(DECAY_TARGET) / FAST_DECAY_PCT
    min_decay = math.log(DECAY_TARGET) / SLOW_DECAY_PCT
    return jnp.abs(jnp.linspace(min_decay, max_decay, width, dtype=F32))[None, :]


def kernel(x_prompt, x_sample, norm_mix, w_in, gm_ln_g, gm_ln_b, gm_ws, gm_bs, hy_conv_w, hy_conv_b, hy_f_w1, hy_f_b1, hy_f_freq1, hy_f_w2, hy_f_b2, hy_f_freq2, hy_f_w3, hy_f_b3, hy_f_freq3, hy_f_w4, hy_skip, w_a_out, w_b_out, w_o, norm_ffn, peer_wq, peer_keys, peer_u, peer_v, final_norm):
    depth = w_in.shape[0]
    d = x_prompt.shape[-1]
    width_b = hy_skip.shape[-1]
    row = lambda a: a.reshape(1, -1)

    w_in_b = w_in.astype(BF16)
    ws_b = gm_ws.astype(BF16)
    bs_bcast = jnp.broadcast_to(gm_bs[..., None], gm_bs.shape + (w_a_out.shape[1] // A_GROUPS,))
    w_a_b, w_b_b, w_o_b = w_a_out.astype(BF16), w_b_out.astype(BF16), w_o.astype(BF16)
    wq_t = jnp.swapaxes(peer_wq, 1, 2)
    u_b = peer_u.astype(BF16)
    v_t = jnp.swapaxes(peer_v, 1, 2).astype(BF16)
    w1_pad = jnp.pad(hy_f_w1, ((0, 0), (0, LANE - POS_EMB), (0, 0)))
    absd = jnp.tile(_decay_rates(width_b), (1, 2 * HY_ORDER))
    bwd = jnp.tile(jnp.repeat(jnp.array([0.0, 1.0], F32), width_b)[None, :], (1, HY_ORDER))

    outs = []
    for x_in in (x_prompt, x_sample):
        batch, seq, _ = x_in.shape
        m_fwd, m_inv, twc, tws = _fft_consts(seq)
        twc32, tws32 = jnp.asarray(twc, F32), jnp.asarray(tws, F32)
        m_fwd_hi = m_fwd.astype(BF16)
        m_fwd16, m_inv16 = jnp.asarray(m_fwd_hi), jnp.asarray(m_inv, BF16)
        m_fwd_lo = jnp.asarray(m_fwd - m_fwd_hi.astype(np.float64), BF16)
        feat = _positional_features(seq)
        x = x_in.reshape(batch * seq, d)
        for l in range(depth):
            yag, gb, zb = _mix_in(x, row(norm_mix[l]), w_in_b[l], row(gm_ln_g[l]), row(gm_ln_b[l]),
                                  ws_b[l], bs_bcast[l], w_a_b[l])
            h_time, h_norm = _filter_time(feat, w1_pad[l], row(hy_f_b1[l]), row(hy_f_freq1[l]),
                                          hy_f_w2[l], row(hy_f_b2[l]), row(hy_f_freq2[l]),
                                          hy_f_w3[l], row(hy_f_b3[l]), row(hy_f_freq3[l]),
                                          hy_f_w4[l], absd, bwd)
            kf = _filter_spec(h_time, h_norm, m_fwd16, m_fwd_lo, twc32, tws32)
            yb = _hyena(zb.reshape(batch, seq, -1), hy_conv_w[l], row(hy_conv_b[l]), hy_skip[l], kf,
                        m_fwd16, m_inv16, twc32, tws32)
            x2, xnt = _merge(x, yag, gb, yb.reshape(batch * seq, width_b), w_b_b[l], w_o_b[l],
                             row(norm_ffn[l]))
            s2, e2, thr, w1 = _routing(xnt, wq_t[l], peer_keys[l])
            x = _experts(xnt, s2, e2, thr, w1, u_b[l], v_t[l], x2, row(final_norm),
                         final_norm=(l == depth - 1))
        outs.append(x.reshape(batch, seq, d))
    return tuple(outs)
```
